```python
import math
import jax
import jax.numpy as jnp
from jax import lax
import numpy as np

D_MODEL = 4096
BATCH = 4
SEQ = 2048
DEPTH = 4
DEC_BATCH = 32
DEC_SEQ = 4
PAST_LEN = 8192
PAGE_SIZE = 128

N_META = 16
N_EVEN = (DEPTH + 1) // 2
N_ODD = DEPTH // 2
SSM_WIDTH = D_MODEL // 2
SSM_GROUP = 16
SSM_GROUPS = SSM_WIDTH // SSM_GROUP
SSM_STATE = 64
DT_MIN = 1e-3
DT_MAX = 1e-1
HEAD_DIM = 64
N_Q_HEADS = (D_MODEL // 2) // HEAD_DIM
N_KV_HEADS = 8
KV_GROUP = N_Q_HEADS // N_KV_HEADS
WINDOW = 128
ATT_BLOCK = 128
N_BUCKETS = 32
MAX_DISTANCE = 128
M_HEADS = 8
M_DV = D_MODEL // M_HEADS
M_DK = M_DV // 2
M_CHUNK = 64
D_FF = -(-8 * D_MODEL // (3 * 256)) * 256
ALPHA = (2 * DEPTH) ** 0.25
BETA = (8 * DEPTH) ** -0.25
LN_EPS = 1e-5

E_U = SSM_WIDTH
E_Q = N_Q_HEADS * HEAD_DIM
E_KV = N_KV_HEADS * HEAD_DIM
E_SPLITS = (E_U, E_U + E_Q, E_U + E_Q + E_KV)
E_TOTAL = E_U + E_Q + 2 * E_KV
O_QK = M_HEADS * M_DK
O_V = M_HEADS * M_DV
O_SPLITS = (O_QK, 2 * O_QK, 2 * O_QK + O_V, 2 * O_QK + O_V + D_MODEL)
O_TOTAL = 2 * O_QK + O_V + D_MODEL + 2 * M_HEADS

kernel_name = 'hybrid_s5_swa_mlstm_decoder_step'


def layer_norm(x, g, b):
    xf = x.astype(jnp.float32)
    mu = xf.mean(-1, keepdims=True)
    var = jnp.square(xf - mu).mean(-1, keepdims=True)
    y = (xf - mu) * lax.rsqrt(var + LN_EPS) * g.astype(jnp.float32) + b.astype(jnp.float32)
    return y.astype(x.dtype)


def swiglu(x, wg, wu, wd):
    return (jax.nn.silu(x @ wg) * (x @ wu)) @ wd


def t5_bucket(rel):
    n = jnp.maximum(rel, 0)
    max_exact = N_BUCKETS // 2
    nf = jnp.maximum(n, 1).astype(jnp.float32)
    large = max_exact + (jnp.log(nf / max_exact) / math.log(MAX_DISTANCE / max_exact)
                         * (N_BUCKETS - max_exact)).astype(jnp.int32)
    large = jnp.minimum(large, N_BUCKETS - 1)
    return jnp.where(n < max_exact, n, large)


def attend_with_sinks(q, k, v, rel, valid, sink, rel_table):
    s = jnp.einsum('...qkgd,...skd->...kgqs', q, k).astype(jnp.float32) * (HEAD_DIM ** -0.5)
    bias = rel_table[t5_bucket(rel)].astype(jnp.float32)
    bias = jnp.moveaxis(bias, -1, 0).reshape(N_KV_HEADS, KV_GROUP, *rel.shape)
    s = jnp.where(valid, s + bias, -jnp.inf)
    sink = sink.astype(jnp.float32).reshape(N_KV_HEADS, KV_GROUP, 1, 1)
    m = jnp.maximum(s.max(-1, keepdims=True), sink)
    p = jnp.exp(s - m)
    p = p / (p.sum(-1, keepdims=True) + jnp.exp(sink - m))
    return jnp.einsum('...kgqs,...skd->...qkgd', p.astype(v.dtype), v)


def swa_prompt(q, k, v, sink, rel_table):
    bsz, length = q.shape[:2]
    pad = (-length) % ATT_BLOCK
    padw = ((0, 0), (pad, 0), (0, 0), (0, 0))
    q, k, v = jnp.pad(q, padw), jnp.pad(k, padw), jnp.pad(v, padw)
    nb = (length + pad) // ATT_BLOCK
    qb = q.reshape(bsz, nb, ATT_BLOCK, N_KV_HEADS, KV_GROUP, HEAD_DIM)

    def with_prev(t):
        t = t.reshape(bsz, nb, ATT_BLOCK, N_KV_HEADS, HEAD_DIM)
        prev = jnp.pad(t[:, :-1], ((0, 0), (1, 0), (0, 0), (0, 0), (0, 0)))
        return jnp.concatenate([prev, t], axis=2)

    kk, vv = with_prev(k), with_prev(v)
    rel = (jnp.arange(ATT_BLOCK)[:, None] + ATT_BLOCK) - jnp.arange(2 * ATT_BLOCK)[None, :]
    kpos = (jnp.arange(nb)[:, None] * ATT_BLOCK + jnp.arange(2 * ATT_BLOCK)[None, :]
            - ATT_BLOCK - pad)
    valid = (rel >= 0) & (rel < WINDOW) & (kpos[:, None, :] >= 0)
    o = attend_with_sinks(qb, kk, vv, rel, valid[:, None, None], sink, rel_table)
    return o.reshape(bsz, nb * ATT_BLOCK, N_Q_HEADS * HEAD_DIM)[:, pad:]


def swa_sample(q, k_new, v_new, k_cache, v_cache, sink, rel_table):
    bsz, s_len = q.shape[:2]
    n_cache = k_cache.shape[1]
    kk = jnp.concatenate([k_cache, k_new], axis=1)
    vv = jnp.concatenate([v_cache, v_new], axis=1)
    qpos = PAST_LEN + jnp.arange(s_len)
    kpos = PAST_LEN - n_cache + jnp.arange(n_cache + s_len)
    rel = qpos[:, None] - kpos[None, :]
    valid = (rel >= 0) & (rel < WINDOW)
    qh = q.reshape(bsz, s_len, N_KV_HEADS, KV_GROUP, HEAD_DIM)
    o = attend_with_sinks(qh, kk, vv, rel, valid, sink, rel_table)
    return o.reshape(bsz, s_len, N_Q_HEADS * HEAD_DIM), kk[:, -n_cache:], vv[:, -n_cache:]


def s5_scan(u, h0_re, h0_im, p):
    f32 = jnp.float32
    bsz, length = u.shape[:2]
    ug = u.reshape(bsz, length, SSM_GROUPS, SSM_GROUP).astype(f32)
    lam = lax.complex(p['lam_re'].astype(f32), p['lam_im'].astype(f32))
    delta = jnp.exp(p['log_step'].astype(f32))[:, None]
    lam_bar = jnp.exp(lam * delta)
    b_bar = ((lam_bar - 1.0) / lam)[..., None] * lax.complex(p['b_re'].astype(f32), p['b_im'].astype(f32))
    bu = jnp.einsum('blgc,gpc->blgp', ug, b_bar)
    h0 = lax.complex(h0_re.astype(f32), h0_im.astype(f32))
    bu = bu.at[:, 0].add(lam_bar * h0)
    a = jnp.broadcast_to(lam_bar, bu.shape)

    def combine(x, y):
        a1, b1 = x
        a2, b2 = y
        return a2 * a1, a2 * b1 + b2

    _, h = lax.associative_scan(combine, (a, bu), axis=1)
    c = lax.complex(p['c_re'].astype(f32), p['c_im'].astype(f32))
    y = jnp.einsum('blgp,gcp->blgc', h, c).real + p['d'].astype(f32) * ug
    h_last = h[:, -1]
    return (y.reshape(bsz, length, SSM_WIDTH).astype(u.dtype),
            h_last.real.astype(h0_re.dtype), h_last.imag.astype(h0_re.dtype))


def even_mixer(x, h0_re, h0_im, k_cache, v_cache, p, rel_table, prompt):
    bsz, length, _ = x.shape
    u, q, k, v = jnp.split(x @ p['w_in'], E_SPLITS, axis=-1)
    q = q.reshape(bsz, length, N_Q_HEADS, HEAD_DIM)
    k = k.reshape(bsz, length, N_KV_HEADS, HEAD_DIM)
    v = v.reshape(bsz, length, N_KV_HEADS, HEAD_DIM)
    y_ssm, h_re, h_im = s5_scan(u, h0_re, h0_im, p)
    g = jax.nn.gelu(y_ssm)
    y_ssm = g * jax.nn.sigmoid(g @ p['w_glu'] + p['b_glu'])
    if prompt:
        y_att = swa_prompt(q, k, v, p['sinks'], rel_table)
        k_buf, v_buf = k[:, -WINDOW:], v[:, -WINDOW:]
    else:
        y_att, k_buf, v_buf = swa_sample(q, k, v, k_cache, v_cache, p['sinks'], rel_table)
    out = jnp.concatenate([y_ssm, y_att.astype(x.dtype)], axis=-1) @ p['w_out']
    return out, (h_re, h_im, k_buf, v_buf)


def mlstm_chunk(carry, xs):
    c, n, m = carry
    q, k, v, log_i, log_f = xs
    t_len = q.shape[2]
    b = jnp.cumsum(log_f, axis=-1)
    causal = jnp.tril(jnp.ones((t_len, t_len), bool))
    d_log = jnp.where(causal, b[..., :, None] - b[..., None, :] + log_i[..., None, :], -jnp.inf)
    g = b + m[..., None]
    m_t = jnp.maximum(g, d_log.max(-1))
    w_inter = jnp.exp(g - m_t)
    w_intra = jnp.exp(d_log - m_t[..., None]) * jnp.einsum('bhtd,bhsd->bhts', q, k)
    num = (w_inter[..., None] * jnp.einsum('bhtd,bhde->bhte', q, c)
           + jnp.einsum('bhts,bhse->bhte', w_intra, v))
    den = w_inter * jnp.einsum('bhtd,bhd->bht', q, n) + w_intra.sum(-1)
    h = num / jnp.maximum(jnp.abs(den), jnp.exp(-m_t))[..., None]
    b_last = b[..., -1]
    w_log = b_last[..., None] - b + log_i
    m_new = jnp.maximum(b_last + m, w_log.max(-1))
    decay = jnp.exp(b_last + m - m_new)
    w = jnp.exp(w_log - m_new[..., None])
    c_new = decay[..., None, None] * c + jnp.einsum('bhsd,bhse->bhde', w[..., None] * k, v)
    n_new = decay[..., None] * n + jnp.einsum('bhs,bhsd->bhd', w, k)
    return (c_new, n_new, m_new), h


def mlstm_scan(carry, xs, chunk):
    length = xs[0].shape[2]
    nc = length // chunk

    def to_chunks(t):
        t = t.reshape(*t.shape[:2], nc, chunk, *t.shape[3:])
        return jnp.moveaxis(t, 2, 0)

    carry, h = lax.scan(mlstm_chunk, carry, tuple(to_chunks(t) for t in xs))
    h = jnp.moveaxis(h, 0, 2).reshape(h.shape[1], h.shape[2], length, M_DV)
    return carry, h


def odd_mixer(x, c0, n0, m0, p, prompt):
    f32 = jnp.float32
    bsz, length, _ = x.shape
    q, k, v, o, gates = jnp.split(x @ p['w_in'], O_SPLITS, axis=-1)

    def heads(t, d):
        return t.reshape(bsz, length, M_HEADS, d).transpose(0, 2, 1, 3).astype(f32)

    q = heads(q, M_DK) * (M_DK ** -0.5)
    k = heads(k, M_DK)
    v = heads(v, M_DV)
    gates = (gates + p['b_gate']).astype(f32).reshape(bsz, length, 2, M_HEADS).transpose(2, 0, 3, 1)
    log_i = gates[0]
    log_f = jax.nn.log_sigmoid(gates[1])
    segments = ((0, N_META, N_META), (N_META, length, M_CHUNK)) if prompt else ((0, length, length),)
    carry = (c0.astype(f32), n0.astype(f32), m0.astype(f32))
    hs = []
    for start, stop, chunk in segments:
        seg = tuple(t[:, :, start:stop] for t in (q, k, v, log_i, log_f))
        carry, h = mlstm_scan(carry, seg, chunk)
        hs.append(h)
    h = jnp.concatenate(hs, axis=2)
    mu = h.mean(-1, keepdims=True)
    var = jnp.square(h - mu).mean(-1, keepdims=True)
    h = (h - mu) * lax.rsqrt(var + LN_EPS) * p['norm_g'].astype(f32)[None, :, None, :]
    h = h.transpose(0, 2, 1, 3).reshape(bsz, length, O_V)
    out = (h * jax.nn.sigmoid(o.astype(f32))).astype(x.dtype) @ p['w_out']
    c_n, n_n, m_n = carry
    return out, (c_n.astype(c0.dtype), n_n.astype(n0.dtype), m_n.astype(m0.dtype))


def setup_inputs(seed: int = 0) -> dict:
    key = jax.random.key(seed)
    keys = iter(jax.random.split(key, 48))
    f32 = jnp.float32

    def nrm(shape, scale=1.0):
        return scale * jax.random.normal(next(keys), shape, f32)

    def gain(shape):
        return 1.0 + nrm(shape, 0.01)

    e_col = jnp.concatenate([jnp.ones((E_U + E_Q + E_KV,), f32), jnp.full((E_KV,), BETA, f32)])
    o_col = jnp.concatenate([jnp.ones((2 * O_QK,), f32), jnp.full((O_V,), BETA, f32),
                             jnp.ones((D_MODEL + 2 * M_HEADS,), f32)])
    lam_im = math.pi * jnp.arange(SSM_STATE, dtype=f32) + nrm((N_EVEN, SSM_GROUPS, SSM_STATE), 0.01)
    log_step = jax.random.uniform(next(keys), (N_EVEN, SSM_GROUPS), dtype=f32,
                                  minval=math.log(DT_MIN), maxval=math.log(DT_MAX))
    b_gate = jnp.concatenate([nrm((N_ODD, M_HEADS), 0.1), 3.0 + nrm((N_ODD, M_HEADS), 0.5)], axis=-1)
    return {
        'x_prompt': nrm((BATCH, SEQ, D_MODEL)),
        'x_sample': nrm((DEC_BATCH, DEC_SEQ, D_MODEL)),
        'state_ssm_re': nrm((N_EVEN, DEC_BATCH, SSM_GROUPS, SSM_STATE), 0.1),
        'state_ssm_im': nrm((N_EVEN, DEC_BATCH, SSM_GROUPS, SSM_STATE), 0.1),
        'cache_swa_k': nrm((N_EVEN, DEC_BATCH, WINDOW, N_KV_HEADS, HEAD_DIM)),
        'cache_swa_v': nrm((N_EVEN, DEC_BATCH, WINDOW, N_KV_HEADS, HEAD_DIM)),
        'state_mlstm_c': nrm((N_ODD, DEC_BATCH, M_HEADS, M_DK, M_DV)),
        'state_mlstm_n': nrm((N_ODD, DEC_BATCH, M_HEADS, M_DK)),
        'state_mlstm_m': nrm((N_ODD, DEC_BATCH, M_HEADS), 0.5),
        'meta_tokens': nrm((N_META, D_MODEL)),
        'rel_bias_table': nrm((N_BUCKETS, N_Q_HEADS), 0.5),
        'w_in_even': nrm((N_EVEN, D_MODEL, E_TOTAL), D_MODEL ** -0.5) * e_col,
        'ssm_lam_re': -0.5 + nrm((N_EVEN, SSM_GROUPS, SSM_STATE), 0.01),
        'ssm_lam_im': lam_im,
        'ssm_b_re': nrm((N_EVEN, SSM_GROUPS, SSM_STATE, SSM_GROUP), (2 * SSM_GROUP) ** -0.5),
        'ssm_b_im': nrm((N_EVEN, SSM_GROUPS, SSM_STATE, SSM_GROUP), (2 * SSM_GROUP) ** -0.5),
        'ssm_c_re': nrm((N_EVEN, SSM_GROUPS, SSM_GROUP, SSM_STATE), (2 * SSM_STATE) ** -0.5),
        'ssm_c_im': nrm((N_EVEN, SSM_GROUPS, SSM_GROUP, SSM_STATE), (2 * SSM_STATE) ** -0.5),
        'ssm_d': nrm((N_EVEN, SSM_GROUPS, SSM_GROUP)),
        'ssm_log_step': log_step,
        'w_glu': nrm((N_EVEN, SSM_WIDTH, SSM_WIDTH), SSM_WIDTH ** -0.5),
        'b_glu': nrm((N_EVEN, SSM_WIDTH), 0.01),
        'attn_sinks': nrm((N_EVEN, N_Q_HEADS)),
        'w_out_even': nrm((N_EVEN, SSM_WIDTH + E_Q, D_MODEL), BETA * (SSM_WIDTH + E_Q) ** -0.5),
        'w_in_odd': nrm((N_ODD, D_MODEL, O_TOTAL), D_MODEL ** -0.5) * o_col,
        'b_gate_odd': b_gate,
        'mlstm_norm_g': gain((N_ODD, M_HEADS, M_DV)),
        'w_out_odd': nrm((N_ODD, O_V, D_MODEL), BETA * O_V ** -0.5),
        'ln_mix_g': gain((DEPTH, D_MODEL)),
        'ln_mix_b': nrm((DEPTH, D_MODEL), 0.01),
        'ln_ffn_g': gain((DEPTH, D_MODEL)),
        'ln_ffn_b': nrm((DEPTH, D_MODEL), 0.01),
        'w_ffn_gate': nrm((DEPTH, D_MODEL, D_FF), D_MODEL ** -0.5),
        'w_ffn_up': nrm((DEPTH, D_MODEL, D_FF), BETA * D_MODEL ** -0.5),
        'w_ffn_down': nrm((DEPTH, D_FF, D_MODEL), BETA * D_FF ** -0.5),
    }


def reference(x_prompt, x_sample, state_ssm_re, state_ssm_im, cache_swa_k, cache_swa_v,
              state_mlstm_c, state_mlstm_n, state_mlstm_m, meta_tokens, rel_bias_table,
              w_in_even, ssm_lam_re, ssm_lam_im, ssm_b_re, ssm_b_im, ssm_c_re, ssm_c_im,
              ssm_d, ssm_log_step, w_glu, b_glu, attn_sinks, w_out_even,
              w_in_odd, b_gate_odd, mlstm_norm_g, w_out_odd,
              ln_mix_g, ln_mix_b, ln_ffn_g, ln_ffn_b, w_ffn_gate, w_ffn_up, w_ffn_down):
    even_p = [dict(w_in=w_in_even[e], lam_re=ssm_lam_re[e], lam_im=ssm_lam_im[e],
                   b_re=ssm_b_re[e], b_im=ssm_b_im[e], c_re=ssm_c_re[e], c_im=ssm_c_im[e],
                   d=ssm_d[e], log_step=ssm_log_step[e], w_glu=w_glu[e], b_glu=b_glu[e],
                   sinks=attn_sinks[e], w_out=w_out_even[e]) for e in range(N_EVEN)]
    odd_p = [dict(w_in=w_in_odd[o], b_gate=b_gate_odd[o], norm_g=mlstm_norm_g[o],
                  w_out=w_out_odd[o]) for o in range(N_ODD)]

    def trunk(x, prompt, ssm_re, ssm_im, swa_k, swa_v, ml_c, ml_n, ml_m):
        bsz = x.shape[0]
        new = ([], [], [], [], [], [], [])
        for layer in range(DEPTH):
            i = layer // 2
            if layer % 2 == 0:
                if prompt:
                    h_re = jnp.zeros((bsz, SSM_GROUPS, SSM_STATE), x.dtype)
                    h_im = h_re
                    kc = vc = None
                else:
                    h_re, h_im, kc, vc = ssm_re[i], ssm_im[i], swa_k[i], swa_v[i]
                mix, st = even_mixer(x, h_re, h_im, kc, vc, even_p[i], rel_bias_table, prompt)
                for lst, s in zip(new[:4], st):
                    lst.append(s)
            else:
                if prompt:
                    c0 = jnp.zeros((bsz, M_HEADS, M_DK, M_DV), x.dtype)
                    n0 = jnp.zeros((bsz, M_HEADS, M_DK), x.dtype)
                    m0 = jnp.zeros((bsz, M_HEADS), x.dtype)
                else:
                    c0, n0, m0 = ml_c[i], ml_n[i], ml_m[i]
                mix, st = odd_mixer(x, c0, n0, m0, odd_p[i], prompt)
                for lst, s in zip(new[4:], st):
                    lst.append(s)
            x = layer_norm(ALPHA * x + mix, ln_mix_g[layer], ln_mix_b[layer])
            x = layer_norm(ALPHA * x + swiglu(x, w_ffn_gate[layer], w_ffn_up[layer], w_ffn_down[layer]),
                           ln_ffn_g[layer], ln_ffn_b[layer])
        return x, [jnp.stack(lst) for lst in new]

    meta = jnp.broadcast_to(meta_tokens.astype(x_prompt.dtype)[None],
                            (x_prompt.shape[0], N_META, D_MODEL))
    xp = jnp.concatenate([meta, x_prompt], axis=1)
    yp, new_p = trunk(xp, True, None, None, None, None, None, None, None)
    y_prompt = yp[:, N_META:]
    y_sample, new_s = trunk(x_sample, False, state_ssm_re, state_ssm_im, cache_swa_k, cache_swa_v,
                            state_mlstm_c, state_mlstm_n, state_mlstm_m)
    p_ssm_re, p_ssm_im, p_swa_k, p_swa_v, p_ml_c, p_ml_n, p_ml_m = new_p
    s_ssm_re, s_ssm_im, s_swa_k, s_swa_v, s_ml_c, s_ml_n, s_ml_m = new_s
    return (y_prompt, y_sample,
            p_ssm_re, p_ssm_im, p_swa_k, p_swa_v, p_ml_c, p_ml_n, p_ml_m,
            s_ssm_re, s_ssm_im, s_swa_k, s_swa_v, s_ml_c, s_ml_n, s_ml_m)
```

```python
import functools
import math

import jax
import jax.numpy as jnp
from jax import lax
from jax.experimental import pallas as pl
from jax.experimental.pallas import tpu as pltpu

D_MODEL = 4096
BATCH = 4
SEQ = 2048
DEPTH = 4
DEC_BATCH = 32
DEC_SEQ = 4
PAST_LEN = 8192
N_META = 16
N_EVEN = (DEPTH + 1) // 2
N_ODD = DEPTH // 2
SSM_WIDTH = D_MODEL // 2
SSM_GROUP = 16
SSM_GROUPS = SSM_WIDTH // SSM_GROUP
SSM_STATE = 64
HEAD_DIM = 64
N_Q_HEADS = (D_MODEL // 2) // HEAD_DIM
N_KV_HEADS = 8
KV_GROUP = N_Q_HEADS // N_KV_HEADS
WINDOW = 128
ATT_BLOCK = 128
N_BUCKETS = 32
MAX_DISTANCE = 128
M_HEADS = 8
M_DV = D_MODEL // M_HEADS
M_DK = M_DV // 2
M_CHUNK = 64
D_FF = -(-8 * D_MODEL // (3 * 256)) * 256
ALPHA = (2 * DEPTH) ** 0.25
LN_EPS = 1e-5
E_U = SSM_WIDTH
E_Q = N_Q_HEADS * HEAD_DIM
E_KV = N_KV_HEADS * HEAD_DIM
E_SPLITS = (E_U, E_U + E_Q, E_U + E_Q + E_KV)
E_TOTAL = E_U + E_Q + 2 * E_KV
O_QK = M_HEADS * M_DK
O_V = M_HEADS * M_DV
O_SPLITS = (O_QK, 2 * O_QK, 2 * O_QK + O_V, 2 * O_QK + O_V + D_MODEL)
O_MAIN = 2 * O_QK + O_V + D_MODEL

VMEM_LIMIT = 56 * 1024 * 1024
BF16 = jnp.bfloat16
F32 = jnp.float32


def _mm_kernel(x_ref, w_ref, o_ref, wb_ref):
    @pl.when(pl.program_id(1) == 0)
    def _cast():
        wb_ref[...] = w_ref[...].astype(BF16)

    o_ref[...] = jnp.dot(x_ref[...], wb_ref[...], preferred_element_type=F32).astype(o_ref.dtype)


def matmul(x, w, layer, *, tm, tn, n_cols=None, out_dtype=F32):
    m, k = x.shape
    n = w.shape[2] if n_cols is None else n_cols
    return pl.pallas_call(
        _mm_kernel,
        grid=(n // tn, m // tm),
        in_specs=[pl.BlockSpec((tm, k), lambda j, i: (i, 0)),
                  pl.BlockSpec((None, k, tn), lambda j, i: (layer, 0, j))],
        out_specs=pl.BlockSpec((tm, tn), lambda j, i: (i, j)),
        out_shape=jax.ShapeDtypeStruct((m, n), out_dtype),
        scratch_shapes=[pltpu.VMEM((k, tn), BF16)],
        compiler_params=pltpu.CompilerParams(
            dimension_semantics=("arbitrary", "arbitrary"), vmem_limit_bytes=VMEM_LIMIT),
    )(x, w)


def _ffn1_kernel(x_ref, wg_ref, wu_ref, o_ref, wgb_ref, wub_ref):
    @pl.when(pl.program_id(1) == 0)
    def _cast():
        wgb_ref[...] = wg_ref[...].astype(BF16)
        wub_ref[...] = wu_ref[...].astype(BF16)

    x = x_ref[...]
    g = jnp.dot(x, wgb_ref[...], preferred_element_type=F32)
    u = jnp.dot(x, wub_ref[...], preferred_element_type=F32)
    o_ref[...] = (g * jax.nn.sigmoid(g) * u).astype(o_ref.dtype)


def ffn1(x, wg, wu, layer, *, tm, tn):
    m, k = x.shape
    n = wg.shape[2]
    return pl.pallas_call(
        _ffn1_kernel,
        grid=(n // tn, m // tm),
        in_specs=[pl.BlockSpec((tm, k), lambda j, i: (i, 0)),
                  pl.BlockSpec((None, k, tn), lambda j, i: (layer, 0, j)),
                  pl.BlockSpec((None, k, tn), lambda j, i: (layer, 0, j))],
        out_specs=pl.BlockSpec((tm, tn), lambda j, i: (i, j)),
        out_shape=jax.ShapeDtypeStruct((m, n), BF16),
        scratch_shapes=[pltpu.VMEM((k, tn), BF16), pltpu.VMEM((k, tn), BF16)],
        compiler_params=pltpu.CompilerParams(
            dimension_semantics=("arbitrary", "arbitrary"), vmem_limit_bytes=VMEM_LIMIT),
    )(x, wg, wu)


def _mmb_kernel(x_ref, w_ref, o_ref):
    o_ref[...] = jnp.dot(x_ref[...], w_ref[...], preferred_element_type=F32).astype(o_ref.dtype)


def matmul_bf16(x, w, layer, *, tm, tn, out_dtype=F32):
    m, k = x.shape
    n = w.shape[2]
    return pl.pallas_call(
        _mmb_kernel,
        grid=(n // tn, m // tm),
        in_specs=[pl.BlockSpec((tm, k), lambda j, i: (i, 0)),
                  pl.BlockSpec((None, k, tn), lambda j, i: (layer, 0, j))],
        out_specs=pl.BlockSpec((tm, tn), lambda j, i: (i, j)),
        out_shape=jax.ShapeDtypeStruct((m, n), out_dtype),
        compiler_params=pltpu.CompilerParams(
            dimension_semantics=("arbitrary", "arbitrary"), vmem_limit_bytes=VMEM_LIMIT),
    )(x, w)


def layer_norm(x, g, b):
    mu = x.mean(-1, keepdims=True)
    var = jnp.square(x - mu).mean(-1, keepdims=True)
    return (x - mu) * lax.rsqrt(var + LN_EPS) * g + b


def t5_bucket(rel):
    n = jnp.maximum(rel, 0)
    max_exact = N_BUCKETS // 2
    nf = jnp.maximum(n, 1).astype(jnp.float32)
    large = max_exact + (jnp.log(nf / max_exact) / math.log(MAX_DISTANCE / max_exact)
                         * (N_BUCKETS - max_exact)).astype(jnp.int32)
    large = jnp.minimum(large, N_BUCKETS - 1)
    return jnp.where(n < max_exact, n, large)


def attend_with_sinks(q, k, v, rel, valid, sink, rel_table):
    s = jnp.einsum('...qkgd,...skd->...kgqs', q, k).astype(jnp.float32) * (HEAD_DIM ** -0.5)
    bias = rel_table[t5_bucket(rel)].astype(jnp.float32)
    bias = jnp.moveaxis(bias, -1, 0).reshape(N_KV_HEADS, KV_GROUP, *rel.shape)
    s = jnp.where(valid, s + bias, -jnp.inf)
    sink = sink.astype(jnp.float32).reshape(N_KV_HEADS, KV_GROUP, 1, 1)
    m = jnp.maximum(s.max(-1, keepdims=True), sink)
    p = jnp.exp(s - m)
    p = p / (p.sum(-1, keepdims=True) + jnp.exp(sink - m))
    return jnp.einsum('...kgqs,...skd->...qkgd', p.astype(v.dtype), v)


def swa_prompt(q, k, v, sink, rel_table):
    bsz, length = q.shape[:2]
    pad = (-length) % ATT_BLOCK
    padw = ((0, 0), (pad, 0), (0, 0), (0, 0))
    q, k, v = jnp.pad(q, padw), jnp.pad(k, padw), jnp.pad(v, padw)
    nb = (length + pad) // ATT_BLOCK
    qb = q.reshape(bsz, nb, ATT_BLOCK, N_KV_HEADS, KV_GROUP, HEAD_DIM)

    def with_prev(t):
        t = t.reshape(bsz, nb, ATT_BLOCK, N_KV_HEADS, HEAD_DIM)
        prev = jnp.pad(t[:, :-1], ((0, 0), (1, 0), (0, 0), (0, 0), (0, 0)))
        return jnp.concatenate([prev, t], axis=2)

    kk, vv = with_prev(k), with_prev(v)
    rel = (jnp.arange(ATT_BLOCK)[:, None] + ATT_BLOCK) - jnp.arange(2 * ATT_BLOCK)[None, :]
    kpos = (jnp.arange(nb)[:, None] * ATT_BLOCK + jnp.arange(2 * ATT_BLOCK)[None, :]
            - ATT_BLOCK - pad)
    valid = (rel >= 0) & (rel < WINDOW) & (kpos[:, None, :] >= 0)
    o = attend_with_sinks(qb, kk, vv, rel, valid[:, None, None], sink, rel_table)
    return o.reshape(bsz, nb * ATT_BLOCK, N_Q_HEADS * HEAD_DIM)[:, pad:]


def swa_sample(q, k_new, v_new, k_cache, v_cache, sink, rel_table):
    bsz, s_len = q.shape[:2]
    n_cache = k_cache.shape[1]
    kk = jnp.concatenate([k_cache, k_new], axis=1)
    vv = jnp.concatenate([v_cache, v_new], axis=1)
    qpos = PAST_LEN + jnp.arange(s_len)
    kpos = PAST_LEN - n_cache + jnp.arange(n_cache + s_len)
    rel = qpos[:, None] - kpos[None, :]
    valid = (rel >= 0) & (rel < WINDOW)
    qh = q.reshape(bsz, s_len, N_KV_HEADS, KV_GROUP, HEAD_DIM)
    o = attend_with_sinks(qh, kk, vv, rel, valid, sink, rel_table)
    return o.reshape(bsz, s_len, N_Q_HEADS * HEAD_DIM), kk[:, -n_cache:], vv[:, -n_cache:]


def s5_scan(u, h0_re, h0_im, p):
    f32 = jnp.float32
    bsz, length = u.shape[:2]
    ug = u.reshape(bsz, length, SSM_GROUPS, SSM_GROUP).astype(f32)
    lam = lax.complex(p['lam_re'], p['lam_im'])
    delta = jnp.exp(p['log_step'])[:, None]
    lam_bar = jnp.exp(lam * delta)
    b_bar = ((lam_bar - 1.0) / lam)[..., None] * lax.complex(p['b_re'], p['b_im'])
    bu = jnp.einsum('blgc,gpc->blgp', ug, b_bar)
    h0 = lax.complex(h0_re, h0_im)
    bu = bu.at[:, 0].add(lam_bar * h0)
    a = jnp.broadcast_to(lam_bar, bu.shape)

    def combine(x, y):
        a1, b1 = x
        a2, b2 = y
        return a2 * a1, a2 * b1 + b2

    _, h = lax.associative_scan(combine, (a, bu), axis=1)
    c = lax.complex(p['c_re'], p['c_im'])
    y = jnp.einsum('blgp,gcp->blgc', h, c).real + p['d'] * ug
    h_last = h[:, -1]
    return y.reshape(bsz, length, SSM_WIDTH), h_last.real, h_last.imag


def mlstm_chunk(carry, xs):
    c, n, m = carry
    q, k, v, log_i, log_f = xs
    t_len = q.shape[2]
    b = jnp.cumsum(log_f, axis=-1)
    causal = jnp.tril(jnp.ones((t_len, t_len), bool))
    d_log = jnp.where(causal, b[..., :, None] - b[..., None, :] + log_i[..., None, :], -jnp.inf)
    g = b + m[..., None]
    m_t = jnp.maximum(g, d_log.max(-1))
    w_inter = jnp.exp(g - m_t)
    w_intra = jnp.exp(d_log - m_t[..., None]) * jnp.einsum('bhtd,bhsd->bhts', q, k)
    num = (w_inter[..., None] * jnp.einsum('bhtd,bhde->bhte', q, c)
           + jnp.einsum('bhts,bhse->bhte', w_intra, v))
    den = w_inter * jnp.einsum('bhtd,bhd->bht', q, n) + w_intra.sum(-1)
    h = num / jnp.maximum(jnp.abs(den), jnp.exp(-m_t))[..., None]
    b_last = b[..., -1]
    w_log = b_last[..., None] - b + log_i
    m_new = jnp.maximum(b_last + m, w_log.max(-1))
    decay = jnp.exp(b_last + m - m_new)
    w = jnp.exp(w_log - m_new[..., None])
    c_new = decay[..., None, None] * c + jnp.einsum('bhsd,bhse->bhde', w[..., None] * k, v)
    n_new = decay[..., None] * n + jnp.einsum('bhs,bhsd->bhd', w, k)
    return (c_new, n_new, m_new), h


def mlstm_scan(carry, xs, chunk):
    length = xs[0].shape[2]
    nc = length // chunk

    def to_chunks(t):
        t = t.reshape(*t.shape[:2], nc, chunk, *t.shape[3:])
        return jnp.moveaxis(t, 2, 0)

    carry, h = lax.scan(mlstm_chunk, carry, tuple(to_chunks(t) for t in xs))
    h = jnp.moveaxis(h, 0, 2).reshape(h.shape[1], h.shape[2], length, M_DV)
    return carry, h


def _tiles(m):
    if m == DEC_BATCH * DEC_SEQ:
        return 128, 128
    return 1376, 192


def kernel(x_prompt, x_sample, state_ssm_re, state_ssm_im, cache_swa_k, cache_swa_v, state_mlstm_c, state_mlstm_n, state_mlstm_m, meta_tokens, rel_bias_table, w_in_even, ssm_lam_re, ssm_lam_im, ssm_b_re, ssm_b_im, ssm_c_re, ssm_c_im, ssm_d, ssm_log_step, w_glu, b_glu, attn_sinks, w_out_even, w_in_odd, b_gate_odd, mlstm_norm_g, w_out_odd, ln_mix_g, ln_mix_b, ln_ffn_g, ln_ffn_b, w_ffn_gate, w_ffn_up, w_ffn_down):
    wd_bf = w_ffn_down.astype(BF16)
    w_gates = jnp.pad(w_in_odd[:, :, O_MAIN:], ((0, 0), (0, 0), (0, 128 - 2 * M_HEADS)))

    def even_mixer(x2, bsz, length, i, h0_re, h0_im, kc, vc, prompt):
        tm, _ = _tiles(x2.shape[0])
        xb = x2.astype(BF16)
        proj = matmul(xb, w_in_even, i, tm=tm, tn=512).reshape(bsz, length, E_TOTAL)
        u, q, k, v = jnp.split(proj, E_SPLITS, axis=-1)
        q = q.reshape(bsz, length, N_Q_HEADS, HEAD_DIM)
        k = k.reshape(bsz, length, N_KV_HEADS, HEAD_DIM)
        v = v.reshape(bsz, length, N_KV_HEADS, HEAD_DIM)
        p = dict(lam_re=ssm_lam_re[i], lam_im=ssm_lam_im[i], b_re=ssm_b_re[i], b_im=ssm_b_im[i],
                 c_re=ssm_c_re[i], c_im=ssm_c_im[i], d=ssm_d[i], log_step=ssm_log_step[i])
        y_ssm, h_re, h_im = s5_scan(u, h0_re, h0_im, p)
        g = jax.nn.gelu(y_ssm).reshape(-1, SSM_WIDTH)
        z = matmul(g.astype(BF16), w_glu, i, tm=tm, tn=512) + b_glu[i]
        y_ssm = g * jax.nn.sigmoid(z)
        if prompt:
            y_att = swa_prompt(q, k, v, attn_sinks[i], rel_bias_table)
            k_buf, v_buf = k[:, -WINDOW:], v[:, -WINDOW:]
        else:
            y_att, k_buf, v_buf = swa_sample(q, k, v, kc, vc, attn_sinks[i], rel_bias_table)
        cat = jnp.concatenate([y_ssm, y_att.reshape(-1, E_Q)], axis=-1).astype(BF16)
        out = matmul(cat, w_out_even, i, tm=tm, tn=512)
        return out, (h_re, h_im, k_buf, v_buf)

    def odd_mixer(x2, bsz, length, i, c0, n0, m0, prompt):
        tm, _ = _tiles(x2.shape[0])
        xb = x2.astype(BF16)
        proj = matmul(xb, w_in_odd, i, tm=tm, tn=512, n_cols=O_MAIN).reshape(bsz, length, O_MAIN)
        gates = matmul(xb, w_gates, i, tm=tm, tn=128)[:, :2 * M_HEADS].reshape(bsz, length, 2 * M_HEADS)
        q, k, v, o = jnp.split(proj, O_SPLITS[:3], axis=-1)

        def heads(t, d):
            return t.reshape(bsz, length, M_HEADS, d).transpose(0, 2, 1, 3)

        q = heads(q, M_DK) * (M_DK ** -0.5)
        k = heads(k, M_DK)
        v = heads(v, M_DV)
        gates = (gates + b_gate_odd[i]).reshape(bsz, length, 2, M_HEADS).transpose(2, 0, 3, 1)
        log_i = gates[0]
        log_f = jax.nn.log_sigmoid(gates[1])
        segments = ((0, N_META, N_META), (N_META, length, M_CHUNK)) if prompt else ((0, length, length),)
        carry = (c0, n0, m0)
        hs = []
        for start, stop, chunk in segments:
            seg = tuple(t[:, :, start:stop] for t in (q, k, v, log_i, log_f))
            carry, h = mlstm_scan(carry, seg, chunk)
            hs.append(h)
        h = jnp.concatenate(hs, axis=2)
        mu = h.mean(-1, keepdims=True)
        var = jnp.square(h - mu).mean(-1, keepdims=True)
        h = (h - mu) * lax.rsqrt(var + LN_EPS) * mlstm_norm_g[i][None, :, None, :]
        h = h.transpose(0, 2, 1, 3).reshape(bsz * length, O_V)
        hg = (h * jax.nn.sigmoid(o.reshape(bsz * length, O_V))).astype(BF16)
        out = matmul(hg, w_out_odd, i, tm=tm, tn=512)
        return out, carry

    def trunk(x, prompt, ssm_re, ssm_im, swa_k, swa_v, ml_c, ml_n, ml_m):
        bsz, length, _ = x.shape
        x2 = x.reshape(bsz * length, D_MODEL)
        tm, tm2 = _tiles(x2.shape[0])
        new = ([], [], [], [], [], [], [])
        for layer in range(DEPTH):
            i = layer // 2
            if layer % 2 == 0:
                if prompt:
                    h_re = jnp.zeros((bsz, SSM_GROUPS, SSM_STATE), F32)
                    h_im = h_re
                    kc = vc = None
                else:
                    h_re, h_im, kc, vc = ssm_re[i], ssm_im[i], swa_k[i], swa_v[i]
                mix, st = even_mixer(x2, bsz, length, i, h_re, h_im, kc, vc, prompt)
                for lst, s in zip(new[:4], st):
                    lst.append(s)
            else:
                if prompt:
                    c0 = jnp.zeros((bsz, M_HEADS, M_DK, M_DV), F32)
                    n0 = jnp.zeros((bsz, M_HEADS, M_DK), F32)
                    m0 = jnp.zeros((bsz, M_HEADS), F32)
                else:
                    c0, n0, m0 = ml_c[i], ml_n[i], ml_m[i]
                mix, st = odd_mixer(x2, bsz, length, i, c0, n0, m0, prompt)
                for lst, s in zip(new[4:], st):
                    lst.append(s)
            x2 = layer_norm(ALPHA * x2 + mix, ln_mix_g[layer], ln_mix_b[layer])
            hmid = ffn1(x2.astype(BF16), w_ffn_gate, w_ffn_up, layer, tm=tm, tn=256)
            y = matmul_bf16(hmid, wd_bf, layer, tm=tm2, tn=512)
            x2 = layer_norm(ALPHA * x2 + y, ln_ffn_g[layer], ln_ffn_b[layer])
        return x2.reshape(bsz, length, D_MODEL), [jnp.stack(lst) for lst in new]

    meta = jnp.broadcast_to(meta_tokens[None], (BATCH, N_META, D_MODEL))
    xp = jnp.concatenate([meta, x_prompt], axis=1)
    yp, new_p = trunk(xp, True, None, None, None, None, None, None, None)
    y_prompt = yp[:, N_META:]
    y_sample, new_s = trunk(x_sample, False, state_ssm_re, state_ssm_im, cache_swa_k, cache_swa_v,
                            state_mlstm_c, state_mlstm_n, state_mlstm_m)
    return (y_prompt, y_sample, *new_p, *new_s)
```
